```python
import jax
import jax.numpy as jnp
from jax import lax
import numpy as np


D_MODEL = 4096
BATCH = 1
SEQ = 16384
DEPTH = 4

CTX_LEN = 256
GRID_W = 64
EPS = 1e-6

MLA_HEADS = 12
MLA_NOPE = 128
MLA_ROPE = 64
MLA_V = 128
MLA_Q_RANK = 1024
MLA_KV_RANK = 512
MLA_WIDTH = MLA_HEADS * MLA_V
MLA_SCALE = (MLA_NOPE + MLA_ROPE) ** -0.5
ROPE_BASE = 10000.0
Q_BLOCK = 128

ML_HEADS = 8
ML_DQK = 96
ML_DV = 192
ML_WIDTH = ML_HEADS * ML_DV
ML_CHUNK = 64

POOL_WINDOWS = (2, 4, 8, 16)
POOL_GROUPS = 4
POOL_GC = 256
POOL_WIDTH = POOL_GROUPS * POOL_GC

MIX_WIDTH = MLA_WIDTH + ML_WIDTH + POOL_WIDTH

IN_SIZES = (MLA_Q_RANK, MLA_KV_RANK, MLA_ROPE,
            ML_HEADS * ML_DQK, ML_HEADS * ML_DQK, ML_WIDTH, ML_WIDTH,
            2 * ML_HEADS, 2 * ML_HEADS, POOL_WIDTH, MIX_WIDTH)
IN_OFFSETS = tuple(int(v) for v in np.cumsum(IN_SIZES)[:-1])
N_IN = int(sum(IN_SIZES))

kernel_name = "hybrid_mla_mlstm_pool_dit"


def rms_norm(x, w):
    xf = x.astype(jnp.float32)
    y = xf * lax.rsqrt(jnp.mean(xf * xf, axis=-1, keepdims=True) + EPS)
    return (y * w.astype(jnp.float32)).astype(x.dtype)


def axial_rope(rows):
    row = jnp.repeat(jnp.arange(rows, dtype=jnp.float32), GRID_W)
    col = jnp.tile(jnp.arange(GRID_W, dtype=jnp.float32), rows)
    n_freq = MLA_ROPE // 4
    inv = jnp.power(ROPE_BASE, -jnp.arange(n_freq, dtype=jnp.float32) / n_freq)
    ang = jnp.concatenate([row[:, None] * inv, col[:, None] * inv], axis=-1)
    return jnp.cos(ang), jnp.sin(ang)


def apply_rope(x, cos, sin):
    xf = x.astype(jnp.float32)
    x1, x2 = jnp.split(xf, 2, axis=-1)
    return jnp.concatenate([x1 * cos - x2 * sin, x1 * sin + x2 * cos], axis=-1).astype(x.dtype)


def mla_project(cq, ckv, krope, q_norm_w, kv_norm_w, w_uq, w_ukv, rope):
    B, L, _ = cq.shape
    q = (rms_norm(cq, q_norm_w) @ w_uq).reshape(B, L, MLA_HEADS, MLA_NOPE + MLA_ROPE)
    kv = (rms_norm(ckv, kv_norm_w) @ w_ukv).reshape(B, L, MLA_HEADS, MLA_NOPE + MLA_V)
    q_nope, q_rope = q[..., :MLA_NOPE], q[..., MLA_NOPE:]
    k_nope, v = kv[..., :MLA_NOPE], kv[..., MLA_NOPE:]
    if rope is not None:
        cos, sin = rope
        q_rope = apply_rope(q_rope, cos[:, None], sin[:, None])
        krope = apply_rope(krope, cos, sin)
    k_rope = jnp.broadcast_to(krope[:, :, None, :], (B, L, MLA_HEADS, MLA_ROPE))
    q = jnp.concatenate([q_nope, q_rope], axis=-1)
    k = jnp.concatenate([k_nope, k_rope], axis=-1)
    return q, k, v


def attend(q, k, v):
    s = jnp.einsum("bqhd,bkhd->bhqk", q, k).astype(jnp.float32) * MLA_SCALE
    p = jax.nn.softmax(s, axis=-1).astype(v.dtype)
    return jnp.einsum("bhqk,bkhd->bqhd", p, v)


def blockwise_attention(q, k_all, v_all):
    B, S, H, Dk = q.shape
    nb = S // Q_BLOCK
    qb = jnp.moveaxis(q.reshape(B, nb, Q_BLOCK, H, Dk), 1, 0)
    out = lax.map(lambda qi: attend(qi, k_all, v_all), qb)
    return jnp.moveaxis(out, 0, 1).reshape(B, S, H * MLA_V)


def _to_chunks(a):
    B, L, H = a.shape[:3]
    a = a.reshape((B, L // ML_CHUNK, ML_CHUNK, H) + a.shape[3:])
    return jnp.moveaxis(a, (1, 3), (0, 2))


def mlstm_scan(q, k, v, li, lf, state, with_out):
    B, L, H, _ = q.shape
    causal = jnp.tril(jnp.ones((ML_CHUNK, ML_CHUNK), dtype=bool))

    def step(carry, xs):
        C, n, m = carry
        qc, kc, vc, lic, lfc = xs
        b = jnp.cumsum(lfc, axis=-1)
        b_end = b[..., -1]
        a = b_end[..., None] - b + lic
        m_new = jnp.maximum(b_end + m, jnp.max(a, axis=-1))
        decay = jnp.exp(b_end + m - m_new)
        w = jnp.exp(a - m_new[..., None])
        C_new = decay[..., None, None] * C + jnp.einsum("bht,bhtv,bhtk->bhvk", w, vc, kc)
        n_new = decay[..., None] * n + jnp.einsum("bht,bhtk->bhk", w, kc)
        if not with_out:
            return (C_new, n_new, m_new), None
        logw = jnp.where(causal, b[..., :, None] - b[..., None, :] + lic[..., None, :], -jnp.inf)
        m_t = jnp.maximum(b + m[..., None], jnp.max(logw, axis=-1))
        inter = jnp.exp(b + m[..., None] - m_t)
        s = jnp.einsum("bhtk,bhsk->bhts", qc, kc) * jnp.exp(logw - m_t[..., None])
        num = inter[..., None] * jnp.einsum("bhvk,bhtk->bhtv", C, qc) + jnp.einsum("bhts,bhsv->bhtv", s, vc)
        den = inter * jnp.einsum("bhk,bhtk->bht", n, qc) + jnp.sum(s, axis=-1)
        h = num / jnp.maximum(jnp.abs(den), jnp.exp(-m_t))[..., None]
        return (C_new, n_new, m_new), h

    xs = tuple(_to_chunks(a) for a in (q, k, v, li, lf))
    state, h = lax.scan(step, state, xs)
    if with_out:
        h = jnp.moveaxis(h, (0, 2), (1, 3)).reshape(B, L, H, ML_DV)
    return state, h


def mlstm_streams(mq, mk, mv, mi, mf, b_i, b_f):
    B, L, _ = mq.shape
    q = mq.astype(jnp.float32).reshape(B, L, ML_HEADS, ML_DQK)
    k = mk.astype(jnp.float32).reshape(B, L, ML_HEADS, ML_DQK) * (ML_DQK ** -0.5)
    v = mv.astype(jnp.float32).reshape(B, L, ML_HEADS, ML_DV)
    li = mi.astype(jnp.float32).reshape(B, L, 2, ML_HEADS) + b_i.astype(jnp.float32)
    lf = jax.nn.log_sigmoid(mf.astype(jnp.float32).reshape(B, L, 2, ML_HEADS) + b_f.astype(jnp.float32))
    return q, k, v, li, lf


def mlstm_direction(stream, d, state, with_out):
    q, k, v, li, lf = stream
    li, lf = li[:, :, d], lf[:, :, d]
    if d == 1:
        q, k, v, li, lf = (a[:, ::-1] for a in (q, k, v, li, lf))
    state, h = mlstm_scan(q, k, v, li, lf, state, with_out)
    if with_out and d == 1:
        h = h[:, ::-1]
    return state, h


def mlstm_out(h, o, norm_w):
    B, L = h.shape[:2]
    hn = h * lax.rsqrt(jnp.mean(h * h, axis=-1, keepdims=True) + EPS)
    hn = hn.reshape(B, L, ML_WIDTH) * norm_w.astype(jnp.float32)
    return (hn * jax.nn.sigmoid(o.astype(jnp.float32))).astype(o.dtype)


def mlstm_mixer(lat, ctx, b_i, b_f, norm_w, o_lat, o_ctx, ctx_out):
    lat_s = mlstm_streams(*lat, b_i, b_f)
    ctx_s = mlstm_streams(*ctx, b_i, b_f)
    B = lat_s[0].shape[0]
    zero = (jnp.zeros((B, ML_HEADS, ML_DV, ML_DQK), jnp.float32),
            jnp.zeros((B, ML_HEADS, ML_DQK), jnp.float32),
            jnp.zeros((B, ML_HEADS), jnp.float32))
    h_lat, h_ctx = [], []
    for d in (0, 1):
        st, hc = mlstm_direction(ctx_s, d, zero, ctx_out)
        _, hl = mlstm_direction(lat_s, d, st, True)
        h_lat.append(hl)
        h_ctx.append(hc)
    y_lat = mlstm_out(h_lat[0] + h_lat[1], o_lat, norm_w)
    y_ctx = mlstm_out(h_ctx[0] + h_ctx[1], o_ctx, norm_w) if ctx_out else None
    return y_lat, y_ctx


def pool_mix(u, pool_w, pool_scale):
    B, L, _ = u.shape
    uf = u.astype(jnp.float32).reshape(B, L, POOL_GROUPS, POOL_GC)
    csum = jnp.concatenate([jnp.zeros((B, 1, POOL_GROUPS, POOL_GC), jnp.float32),
                            jnp.cumsum(uf, axis=1)], axis=1)
    t = jnp.arange(L)
    groups = []
    for g, w in enumerate(POOL_WINDOWS):
        lo = jnp.clip(t - w // 2, 0, L)
        hi = jnp.clip(t + w - w // 2, 0, L)
        mean = (csum[:, hi, g] - csum[:, lo, g]) / (hi - lo).astype(jnp.float32)[None, :, None]
        groups.append(mean - uf[:, :, g])
    pooled = jnp.stack(groups, axis=2)
    y = jnp.einsum("blgc,gcd->blgd", pooled, pool_w.astype(jnp.float32)).reshape(B, L, POOL_WIDTH)
    return (y * pool_scale.astype(jnp.float32)).astype(u.dtype)


def mixer_layer(h, hc, rope, w_in, q_norm_w, kv_norm_w, w_uq, w_ukv, b_i, b_f, ml_norm_w,
                pool_w, pool_scale, w_out, ctx_out):
    B, S, _ = h.shape
    cq, ckv, kr, mq, mk, mv, mo, mi, mf, pu, z = jnp.split(h @ w_in, IN_OFFSETS, axis=-1)
    cq_c, ckv_c, kr_c, mq_c, mk_c, mv_c, mo_c, mi_c, mf_c, pu_c, z_c = jnp.split(hc @ w_in, IN_OFFSETS, axis=-1)
    q_l, k_l, v_l = mla_project(cq, ckv, kr, q_norm_w, kv_norm_w, w_uq, w_ukv, rope)
    q_c, k_c, v_c = mla_project(cq_c, ckv_c, kr_c, q_norm_w, kv_norm_w, w_uq, w_ukv, None)
    k_all = jnp.concatenate([k_c, k_l], axis=1)
    v_all = jnp.concatenate([v_c, v_l], axis=1)
    a_lat = blockwise_attention(q_l, k_all, v_all)
    ml_lat, ml_ctx = mlstm_mixer((mq, mk, mv, mi, mf), (mq_c, mk_c, mv_c, mi_c, mf_c),
                                 b_i, b_f, ml_norm_w, mo, mo_c, ctx_out)
    p_lat = pool_mix(pu, pool_w, pool_scale)
    y = (jnp.concatenate([a_lat, ml_lat, p_lat], axis=-1) * jax.nn.silu(z)) @ w_out
    if not ctx_out:
        return y, None
    a_ctx = attend(q_c, k_c, v_c).reshape(B, hc.shape[1], MLA_WIDTH)
    p_ctx = pool_mix(pu_c, pool_w, pool_scale)
    yc = (jnp.concatenate([a_ctx, ml_ctx, p_ctx], axis=-1) * jax.nn.silu(z_c)) @ w_out
    return y, yc


def setup_inputs(seed: int = 0) -> dict:
    key = jax.random.key(seed)
    ks = jax.random.split(key, 19)
    D = D_MODEL

    def nrm(k, shape, s):
        return jax.random.normal(k, shape, jnp.float32) * s

    return {
        "x": nrm(ks[0], (BATCH, SEQ, D), 1.0),
        "c": nrm(ks[1], (BATCH, D), 1.0),
        "ctx": nrm(ks[2], (BATCH, CTX_LEN, D), 1.0),
        "c_ctx": nrm(ks[3], (D,), 1.0),
        "norm_w": 1.0 + nrm(ks[4], (DEPTH, D), 0.05),
        "w_mod": nrm(ks[5], (DEPTH, D, 3 * D), 0.5 * D ** -0.5),
        "b_mod": nrm(ks[6], (DEPTH, 3 * D), 0.02),
        "w_in": nrm(ks[7], (DEPTH, D, N_IN), D ** -0.5),
        "q_norm_w": 1.0 + nrm(ks[8], (DEPTH, MLA_Q_RANK), 0.05),
        "kv_norm_w": 1.0 + nrm(ks[9], (DEPTH, MLA_KV_RANK), 0.05),
        "w_uq": nrm(ks[10], (DEPTH, MLA_Q_RANK, MLA_HEADS * (MLA_NOPE + MLA_ROPE)), MLA_Q_RANK ** -0.5),
        "w_ukv": nrm(ks[11], (DEPTH, MLA_KV_RANK, MLA_HEADS * (MLA_NOPE + MLA_V)), MLA_KV_RANK ** -0.5),
        "ml_b_i": nrm(ks[12], (DEPTH, 2, ML_HEADS), 0.1),
        "ml_b_f": jnp.linspace(3.0, 6.0, ML_HEADS, dtype=jnp.float32) + nrm(ks[13], (DEPTH, 2, ML_HEADS), 0.1),
        "ml_norm_w": 1.0 + nrm(ks[14], (DEPTH, ML_WIDTH), 0.05),
        "pool_w": nrm(ks[15], (DEPTH, POOL_GROUPS, POOL_GC, POOL_GC), POOL_GC ** -0.5),
        "pool_scale": 1.0 + nrm(ks[16], (DEPTH, POOL_WIDTH), 0.05),
        "w_out": nrm(ks[17], (DEPTH, MIX_WIDTH, D), MIX_WIDTH ** -0.5),
        "final_norm_w": 1.0 + nrm(ks[18], (D,), 0.05),
    }


def reference(x, c, ctx, c_ctx, norm_w, w_mod, b_mod, w_in, q_norm_w, kv_norm_w, w_uq, w_ukv,
              ml_b_i, ml_b_f, ml_norm_w, pool_w, pool_scale, w_out, final_norm_w):
    B, S, _ = x.shape
    ROWS = S // GRID_W
    rope = axial_rope(ROWS)
    for l in range(DEPTH):
        ctx_out = l < DEPTH - 1
        mod = jax.nn.silu(c) @ w_mod[l] + b_mod[l]
        mod_c = jax.nn.silu(c_ctx) @ w_mod[l] + b_mod[l]
        shift, scale, gate = jnp.split(mod, 3, axis=-1)
        shift_c, scale_c, gate_c = jnp.split(mod_c, 3, axis=-1)
        h = rms_norm(x, norm_w[l]) * (1.0 + scale[:, None]) + shift[:, None]
        hc = rms_norm(ctx, norm_w[l]) * (1.0 + scale_c) + shift_c
        y, yc = mixer_layer(h, hc, rope, w_in[l], q_norm_w[l], kv_norm_w[l], w_uq[l], w_ukv[l],
                            ml_b_i[l], ml_b_f[l], ml_norm_w[l], pool_w[l], pool_scale[l], w_out[l], ctx_out)
        x = x + gate[:, None] * y
        if ctx_out:
            ctx = ctx + gate_c * yc
    return rms_norm(x, final_norm_w)
```

```python
import functools

import numpy as np
import jax
import jax.numpy as jnp
from jax import lax
from jax.experimental import pallas as pl
from jax.experimental.pallas import tpu as pltpu

F32 = jnp.float32
BF16 = jnp.bfloat16

D_MODEL = 4096
GRID_W = 64
EPS = 1e-6

MLA_HEADS = 12
MLA_NOPE = 128
MLA_ROPE = 64
MLA_V = 128
MLA_DK = MLA_NOPE + MLA_ROPE
MLA_Q_RANK = 1024
MLA_KV_RANK = 512
MLA_WIDTH = MLA_HEADS * MLA_V
MLA_SCALE = MLA_DK ** -0.5
ROPE_BASE = 10000.0
ROPE_HALF = MLA_ROPE // 2

ML_HEADS = 8
ML_DQK = 96
ML_DQK_PAD = 128
ML_DV = 192
ML_WIDTH = ML_HEADS * ML_DV
ML_PAIRS = ML_HEADS // 2
ML_PAIR_W = 2 * ML_DV
ML_QK_W = ML_HEADS * ML_DQK_PAD
ML_K_SCALE = ML_DQK ** -0.5

POOL_WINDOWS = (2, 4, 8, 16)
POOL_GROUPS = 4
POOL_GC = 256
POOL_WIDTH = POOL_GROUPS * POOL_GC
MIX_WIDTH = MLA_WIDTH + ML_WIDTH + POOL_WIDTH

IN_SIZES = (MLA_Q_RANK, MLA_KV_RANK, MLA_ROPE,
            ML_HEADS * ML_DQK, ML_HEADS * ML_DQK, ML_WIDTH, ML_WIDTH,
            2 * ML_HEADS, 2 * ML_HEADS, POOL_WIDTH, MIX_WIDTH)
IN_OFFSETS = tuple(int(v) for v in np.cumsum(IN_SIZES)[:-1])

P_MV = 0
P_MO = P_MV + ML_WIDTH
P_CQ = P_MO + ML_WIDTH
P_Z = P_CQ + MLA_Q_RANK
P_PU = P_Z + MIX_WIDTH
P_MQ = P_PU + POOL_WIDTH
P_MK = P_MQ + ML_QK_W
P_CKV = P_MK + ML_QK_W
P_WIDTH = P_CKV + MLA_KV_RANK
G_KR = 0
G_MI = MLA_ROPE
G_MF = G_MI + 2 * ML_HEADS
G_WIDTH = 128

ROW_BLOCK = 256
ATT_TK = 512
V7X_VMEM_LIMIT = 56 * 1024 * 1024


def _pick(n, candidates):
    for c in candidates:
        if n % c == 0:
            return c
    raise ValueError(f"no tile in {candidates} divides {n}")


def _cparams(sem):
    return pltpu.CompilerParams(dimension_semantics=sem, vmem_limit_bytes=V7X_VMEM_LIMIT)


def _nt_dot(a, b):
    return lax.dot_general(a, b, (((1,), (1,)), ((), ())), preferred_element_type=F32)


def _dot(a, b):
    return jnp.dot(a, b, preferred_element_type=F32)


def _mod_kernel(c_ref, w_ref, b_ref, o_ref):
    c = c_ref[...]
    s = c * jax.nn.sigmoid(c)
    o_ref[...] = _dot(s.astype(BF16), w_ref[...].astype(BF16)) + b_ref[...]


def _modulation(c2, w_mod, b_mod):
    depth, d, n = w_mod.shape
    tn = 512
    return pl.pallas_call(
        _mod_kernel,
        grid=(depth, n // tn),
        in_specs=[pl.BlockSpec((16, d), lambda l, j: (0, 0)),
                  pl.BlockSpec((None, d, tn), lambda l, j: (l, 0, j)),
                  pl.BlockSpec((None, 1, tn), lambda l, j: (l, 0, j))],
        out_specs=pl.BlockSpec((None, 16, tn), lambda l, j: (l, 0, j)),
        out_shape=jax.ShapeDtypeStruct((depth, 16, n), F32),
        compiler_params=_cparams(("arbitrary", "arbitrary")),
        name="modulation",
    )(c2, w_mod, b_mod.reshape(depth, 1, n))


def _modnorm_kernel(x_ref, nw_ref, sc_ref, sh_ref, o_ref):
    x = x_ref[...]
    y = x * lax.rsqrt(jnp.mean(x * x, axis=-1, keepdims=True) + EPS) * nw_ref[...]
    o_ref[...] = (y * (1.0 + sc_ref[...]) + sh_ref[...]).astype(o_ref.dtype)


def _modnorm(xa, nw, scale2, shift2, n_lat):
    r, d = xa.shape
    tm = ROW_BLOCK
    lat_blocks = n_lat // tm
    sel = lambda i: (jnp.where(i >= lat_blocks, 1, 0), 0, 0)
    return pl.pallas_call(
        _modnorm_kernel,
        grid=(r // tm,),
        in_specs=[pl.BlockSpec((tm, d), lambda i: (i, 0)),
                  pl.BlockSpec((1, d), lambda i: (0, 0)),
                  pl.BlockSpec((None, 1, d), sel),
                  pl.BlockSpec((None, 1, d), sel)],
        out_specs=pl.BlockSpec((tm, d), lambda i: (i, 0)),
        out_shape=jax.ShapeDtypeStruct((r, d), BF16),
        compiler_params=_cparams(("arbitrary",)),
        name="modnorm",
    )(xa, nw, scale2, shift2)


def _mm_kernel(a_ref, b_ref, o_ref):
    o_ref[...] = _dot(a_ref[...], b_ref[...]).astype(o_ref.dtype)


def _matmul(a, b, out_dtype):
    m, k = a.shape
    n = b.shape[1]
    tm = _pick(m, (1280, 768, 256))
    tn = _pick(n, (512, 256, 128))
    return pl.pallas_call(
        _mm_kernel,
        grid=(m // tm, n // tn),
        in_specs=[pl.BlockSpec((tm, k), lambda i, j: (i, 0)),
                  pl.BlockSpec((k, tn), lambda i, j: (0, j))],
        out_specs=pl.BlockSpec((tm, tn), lambda i, j: (i, j)),
        out_shape=jax.ShapeDtypeStruct((m, n), out_dtype),
        compiler_params=_cparams(("arbitrary", "arbitrary")),
        name="in_proj",
    )(a, b)


def _gates_kernel(a_ref, w_ref, wt_ref, g_ref, gt_ref):
    a = a_ref[...]
    g_ref[...] = _dot(a, w_ref[...])
    gt_ref[...] = _nt_dot(wt_ref[...], a)


def _small_proj(a, ws, wst):
    m, k = a.shape
    tm = _pick(m, (1280, 768, 256))
    return pl.pallas_call(
        _gates_kernel,
        grid=(m // tm,),
        in_specs=[pl.BlockSpec((tm, k), lambda i: (i, 0)),
                  pl.BlockSpec((k, G_WIDTH), lambda i: (0, 0)),
                  pl.BlockSpec((G_WIDTH, k), lambda i: (0, 0))],
        out_specs=[pl.BlockSpec((tm, G_WIDTH), lambda i: (i, 0)),
                   pl.BlockSpec((G_WIDTH, tm), lambda i: (0, i))],
        out_shape=[jax.ShapeDtypeStruct((m, G_WIDTH), F32),
                   jax.ShapeDtypeStruct((G_WIDTH, m), F32)],
        compiler_params=_cparams(("arbitrary",)),
        name="small_proj",
    )(a, ws, wst)


def _mm_res_kernel(n_lat, a_ref, b_ref, x_ref, g_ref, o_ref):
    tm = a_ref.shape[0]
    y = _dot(a_ref[...], b_ref[...])
    row = pl.program_id(0) * tm + lax.broadcasted_iota(jnp.int32, y.shape, 0)
    g = jnp.where(row >= n_lat, g_ref[1:2, :], g_ref[0:1, :])
    o_ref[...] = x_ref[...] + g * y


def _matmul_residual(a, b, xa, gate2, n_lat):
    m, k = a.shape
    n = b.shape[1]
    tm = _pick(m, (1280, 768, 256))
    tn = _pick(n, (512, 256, 128))
    return pl.pallas_call(
        functools.partial(_mm_res_kernel, n_lat),
        grid=(m // tm, n // tn),
        in_specs=[pl.BlockSpec((tm, k), lambda i, j: (i, 0)),
                  pl.BlockSpec((k, tn), lambda i, j: (0, j)),
                  pl.BlockSpec((tm, tn), lambda i, j: (i, j)),
                  pl.BlockSpec((2, tn), lambda i, j: (0, j))],
        out_specs=pl.BlockSpec((tm, tn), lambda i, j: (i, j)),
        out_shape=jax.ShapeDtypeStruct((m, n), F32),
        input_output_aliases={2: 0},
        compiler_params=_cparams(("arbitrary", "arbitrary")),
        name="out_proj",
    )(a, b, xa, gate2)


def _rms_rows(x, w):
    return x * lax.rsqrt(jnp.mean(x * x, axis=-1, keepdims=True) + EPS) * w


def _q_kernel(cq_ref, nw_ref, wt_ref, cos_ref, sin_ref, o_ref, xn_ref):
    @pl.when(pl.program_id(1) == 0)
    def _():
        xn_ref[...] = _rms_rows(cq_ref[...].astype(F32), nw_ref[...]).astype(BF16)

    qt = _nt_dot(wt_ref[...], xn_ref[...])
    x1 = qt[MLA_NOPE:MLA_NOPE + ROPE_HALF]
    x2 = qt[MLA_NOPE + ROPE_HALF:MLA_DK]
    c = cos_ref[...]
    s = sin_ref[...]
    o_ref[0:MLA_NOPE, :] = (qt[0:MLA_NOPE] * MLA_SCALE).astype(BF16)
    o_ref[MLA_NOPE:MLA_NOPE + ROPE_HALF, :] = ((x1 * c - x2 * s) * MLA_SCALE).astype(BF16)
    o_ref[MLA_NOPE + ROPE_HALF:MLA_DK, :] = ((x1 * s + x2 * c) * MLA_SCALE).astype(BF16)


def _q_proj(p, q_norm_w, wq_t, cos_t, sin_t):
    r = p.shape[0]
    tm = _pick(r, (1280, 768, 256))
    return pl.pallas_call(
        _q_kernel,
        grid=(r // tm, MLA_HEADS),
        in_specs=[pl.BlockSpec((tm, MLA_Q_RANK), lambda i, h: (i, P_CQ // MLA_Q_RANK)),
                  pl.BlockSpec((1, MLA_Q_RANK), lambda i, h: (0, 0)),
                  pl.BlockSpec((None, MLA_DK, MLA_Q_RANK), lambda i, h: (h, 0, 0)),
                  pl.BlockSpec((ROPE_HALF, tm), lambda i, h: (0, i)),
                  pl.BlockSpec((ROPE_HALF, tm), lambda i, h: (0, i))],
        out_specs=pl.BlockSpec((None, MLA_DK, tm), lambda i, h: (h, 0, i)),
        out_shape=jax.ShapeDtypeStruct((MLA_HEADS, MLA_DK, r), BF16),
        scratch_shapes=[pltpu.VMEM((tm, MLA_Q_RANK), BF16)],
        compiler_params=_cparams(("arbitrary", "arbitrary")),
        name="q_proj",
    )(p, q_norm_w, wq_t, cos_t, sin_t)


def _kv_kernel(ckv_ref, nw_ref, wk_ref, wvt_ref, g_ref, cos_ref, sin_ref, k_ref, vt_ref, xn_ref):
    @pl.when(pl.program_id(1) == 0)
    def _():
        xn_ref[...] = _rms_rows(ckv_ref[...].astype(F32), nw_ref[...]).astype(BF16)

    xn = xn_ref[...]
    k_ref[:, 0:MLA_NOPE] = _dot(xn, wk_ref[...]).astype(BF16)
    g = g_ref[...]
    lane = lax.broadcasted_iota(jnp.int32, g.shape, 1)
    partner = jnp.where(lane < ROPE_HALF,
                        pltpu.roll(g, G_WIDTH - ROPE_HALF, axis=1),
                        pltpu.roll(g, ROPE_HALF, axis=1))
    roped = g * cos_ref[...] + partner * sin_ref[...]
    k_ref[:, MLA_NOPE:MLA_DK] = roped[:, 0:MLA_ROPE].astype(BF16)
    vt_ref[...] = _nt_dot(wvt_ref[...], xn).astype(BF16)


def _kv_proj(p, g, kv_norm_w, wk, wv_t, cos128, sin128):
    r = p.shape[0]
    tm = _pick(r, (1280, 768, 256))
    return pl.pallas_call(
        _kv_kernel,
        grid=(r // tm, MLA_HEADS),
        in_specs=[pl.BlockSpec((tm, MLA_KV_RANK), lambda i, h: (i, P_CKV // MLA_KV_RANK)),
                  pl.BlockSpec((1, MLA_KV_RANK), lambda i, h: (0, 0)),
                  pl.BlockSpec((None, MLA_KV_RANK, MLA_NOPE), lambda i, h: (h, 0, 0)),
                  pl.BlockSpec((None, MLA_V, MLA_KV_RANK), lambda i, h: (h, 0, 0)),
                  pl.BlockSpec((tm, G_WIDTH), lambda i, h: (i, 0)),
                  pl.BlockSpec((tm, G_WIDTH), lambda i, h: (i, 0)),
                  pl.BlockSpec((tm, G_WIDTH), lambda i, h: (i, 0))],
        out_specs=[pl.BlockSpec((None, tm, MLA_DK), lambda i, h: (h, i, 0)),
                   pl.BlockSpec((None, MLA_V, tm), lambda i, h: (h, 0, i))],
        out_shape=[jax.ShapeDtypeStruct((MLA_HEADS, r, MLA_DK), BF16),
                   jax.ShapeDtypeStruct((MLA_HEADS, MLA_V, r), BF16)],
        scratch_shapes=[pltpu.VMEM((tm, MLA_KV_RANK), BF16)],
        compiler_params=_cparams(("arbitrary", "arbitrary")),
        name="kv_proj",
    )(p, kv_norm_w, wk, wv_t, g, cos128, sin128)


def _attn_kernel(n_lat, n_ctx, qt_ref, k_ref, vt_ref, o_ref):
    tq = qt_ref.shape[1]
    qt = qt_ref[...]
    is_ctx_block = pl.program_id(1) * tq >= n_lat

    def update(carry, k_c, vt_c):
        m, l, acc = carry
        st = _dot(k_c, qt)
        m_new = jnp.maximum(m, jnp.max(st, axis=0, keepdims=True))
        alpha = jnp.exp(m - m_new)
        pt = jnp.exp(st - m_new)
        l = alpha * l + jnp.sum(pt, axis=0, keepdims=True)
        acc = alpha * acc + _dot(vt_c, pt.astype(BF16))
        return m_new, l, acc

    def body(c, carry):
        start = pl.multiple_of(c * ATT_TK, ATT_TK)
        return update(carry, k_ref[pl.ds(start, ATT_TK), :], vt_ref[:, pl.ds(start, ATT_TK)])

    init = (jnp.full((1, tq), -jnp.inf, F32), jnp.zeros((1, tq), F32), jnp.zeros((MLA_V, tq), F32))
    n_main = jnp.where(is_ctx_block, 0, n_lat // ATT_TK)
    carry = lax.fori_loop(0, n_main, body, init)
    m, l, acc = update(carry, k_ref[n_lat:n_lat + n_ctx, :], vt_ref[:, n_lat:n_lat + n_ctx])
    o_ref[...] = (acc / l).T.astype(o_ref.dtype)


def _attention(qt, k, vt, n_lat):
    heads, dk, r = qt.shape
    tq = ROW_BLOCK
    return pl.pallas_call(
        functools.partial(_attn_kernel, n_lat, r - n_lat),
        grid=(heads, r // tq),
        in_specs=[pl.BlockSpec((None, dk, tq), lambda h, i: (h, 0, i)),
                  pl.BlockSpec((None, r, dk), lambda h, i: (h, 0, 0)),
                  pl.BlockSpec((None, MLA_V, r), lambda h, i: (h, 0, 0))],
        out_specs=pl.BlockSpec((tq, MLA_V), lambda h, i: (i, h)),
        out_shape=jax.ShapeDtypeStruct((r, heads * MLA_V), BF16),
        compiler_params=_cparams(("arbitrary", "arbitrary")),
        name="attention",
    )(qt, k, vt)


def _split3_dot(a, m01, left):
    hi = a.astype(BF16)
    r1 = a - hi.astype(F32)
    mid = r1.astype(BF16)
    lo = (r1 - mid.astype(F32)).astype(BF16)
    if left:
        return _dot(m01, hi) + _dot(m01, mid) + _dot(m01, lo)
    return _dot(hi, m01) + _dot(mid, m01) + _dot(lo, m01)


def _log_sigmoid(x):
    return jnp.minimum(x, 0.0) - jnp.log1p(jnp.exp(-jnp.abs(x)))


def _mlstm_direction(d, q_ref, k_ref, v_ref, g_ref, gt_ref, bi_ref, bf_ref, bit_ref, bft_ref,
                     h_ref, c_ref, n_ref, m_ref):
    t = q_ref.shape[0]
    lo = d * ML_HEADS
    r_i = lax.broadcasted_iota(jnp.int32, (t, t), 0)
    c_i = lax.broadcasted_iota(jnp.int32, (t, t), 1)
    seen = (c_i <= r_i) if d == 0 else (c_i >= r_i)
    tri_col = jnp.where(seen, 1.0, 0.0).astype(BF16)
    seen_t = (r_i <= c_i) if d == 0 else (r_i >= c_i)
    tri_row = jnp.where(seen_t, 1.0, 0.0).astype(BF16)

    g = g_ref[...]
    gt = gt_ref[...]
    li_col = g[:, G_MI + lo:G_MI + lo + ML_HEADS] + bi_ref[:, lo:lo + ML_HEADS]
    lf_col = _log_sigmoid(g[:, G_MF + lo:G_MF + lo + ML_HEADS] + bf_ref[:, lo:lo + ML_HEADS])
    li_row = gt[G_MI + lo:G_MI + lo + ML_HEADS, :] + bit_ref[lo:lo + ML_HEADS, :]
    lf_row = _log_sigmoid(gt[G_MF + lo:G_MF + lo + ML_HEADS, :] + bft_ref[lo:lo + ML_HEADS, :])
    b_col = _split3_dot(lf_col, tri_col, left=True)
    b_row = _split3_dot(lf_row, tri_row, left=False)
    last = t - 1 if d == 0 else 0

    lane = lax.broadcasted_iota(jnp.int32, (1, ML_PAIR_W), 1)
    first = lane < ML_DV
    m_all = m_ref[d]
    m_new_all = []
    for pair in range(ML_PAIRS):
        v2 = v_ref[:, pair * ML_PAIR_W:(pair + 1) * ML_PAIR_W]
        ct2 = c_ref[d, pair]
        ct2_b = ct2.astype(BF16)
        inter_n, sv, den, dec, upd = [], [], [], [], []
        for sub in range(2):
            hd = 2 * pair + sub
            q = q_ref[:, hd * ML_DQK_PAD:(hd + 1) * ML_DQK_PAD]
            k = k_ref[:, hd * ML_DQK_PAD:(hd + 1) * ML_DQK_PAD]
            bc = b_col[:, hd:hd + 1]
            br = b_row[hd:hd + 1, :]
            lic = li_col[:, hd:hd + 1]
            lir = li_row[hd:hd + 1, :]
            m = m_all[:, hd:hd + 1]
            b_end = bc[last:last + 1, :]
            a_col = b_end - bc + lic
            m_new = jnp.maximum(b_end + m, jnp.max(a_col, axis=0, keepdims=True))
            decay = jnp.exp(b_end + m - m_new)
            w_col = jnp.exp(a_col - m_new)
            logw = jnp.where(seen, bc - br + lir, -jnp.inf)
            m_t = jnp.maximum(bc + m, jnp.max(logw, axis=1, keepdims=True))
            inter = jnp.exp(bc + m - m_t)
            s = _nt_dot(q, k) * (jnp.exp(logw - m_t) * ML_K_SCALE)
            n_row = n_ref[d, hd]
            qf = q.astype(F32)
            den_h = inter * jnp.sum(qf * n_row, axis=1, keepdims=True) + jnp.sum(s, axis=1, keepdims=True)
            den.append(jnp.maximum(jnp.abs(den_h), jnp.exp(-m_t)))
            inter_n.append(inter * _dot(q, ct2_b))
            sv.append(_dot(s.astype(BF16), v2))
            kw = k.astype(F32) * (w_col * ML_K_SCALE)
            upd.append(_dot(kw.T.astype(BF16), v2))
            dec.append(decay)
            n_ref[d, hd] = decay * n_row + jnp.sum(kw, axis=0, keepdims=True)
            m_new_all.append(m_new)
        num = jnp.where(first, inter_n[0] + sv[0], inter_n[1] + sv[1])
        h_ref[:, pair * ML_PAIR_W:(pair + 1) * ML_PAIR_W] = num / jnp.where(first, den[0], den[1])
        c_ref[d, pair] = jnp.where(first, dec[0] * ct2 + upd[0], dec[1] * ct2 + upd[1])
    head_lane = lax.broadcasted_iota(jnp.int32, (1, ML_HEADS), 1)
    m_vec = jnp.zeros((1, ML_HEADS), F32)
    for hd, m_new in enumerate(m_new_all):
        m_vec = jnp.where(head_lane == hd, m_new, m_vec)
    m_ref[d] = m_vec


def _mlstm_kernel(qf_ref, kf_ref, vf_ref, gf_ref, gtf_ref, qb_ref, kb_ref, vb_ref, gb_ref, gtb_ref,
                  bi_ref, bf_ref, bit_ref, bft_ref, hf_ref, hb_ref, c_ref, n_ref, m_ref):
    @pl.when(pl.program_id(0) == 0)
    def _():
        c_ref[...] = jnp.zeros_like(c_ref)
        n_ref[...] = jnp.zeros_like(n_ref)
        m_ref[...] = jnp.zeros_like(m_ref)

    common = (bi_ref, bf_ref, bit_ref, bft_ref)
    _mlstm_direction(0, qf_ref, kf_ref, vf_ref, gf_ref, gtf_ref, *common, hf_ref, c_ref, n_ref, m_ref)
    _mlstm_direction(1, qb_ref, kb_ref, vb_ref, gb_ref, gtb_ref, *common, hb_ref, c_ref, n_ref, m_ref)


def _mlstm(p, g, gt, b_i, b_f, n_lat):
    r = p.shape[0]
    t = ROW_BLOCK
    nb = r // t
    lat_b = n_lat // t
    ctx_b = nb - lat_b
    fwd = lambda s: jnp.where(s < ctx_b, lat_b + s, s - ctx_b)
    bwd = lambda s: jnp.where(s < ctx_b, lat_b + ctx_b - 1 - s, lat_b - 1 - (s - ctx_b))

    def specs(order):
        return [pl.BlockSpec((t, ML_QK_W), lambda s: (order(s), P_MQ // ML_QK_W)),
                pl.BlockSpec((t, ML_QK_W), lambda s: (order(s), P_MK // ML_QK_W)),
                pl.BlockSpec((t, ML_WIDTH), lambda s: (order(s), P_MV // ML_WIDTH)),
                pl.BlockSpec((t, G_WIDTH), lambda s: (order(s), 0)),
                pl.BlockSpec((G_WIDTH, t), lambda s: (0, order(s)))]

    nh2 = 2 * ML_HEADS
    bi = b_i.reshape(1, nh2)
    bf = b_f.reshape(1, nh2)
    small = [pl.BlockSpec((1, nh2), lambda s: (0, 0)), pl.BlockSpec((1, nh2), lambda s: (0, 0)),
             pl.BlockSpec((nh2, 1), lambda s: (0, 0)), pl.BlockSpec((nh2, 1), lambda s: (0, 0))]
    return pl.pallas_call(
        _mlstm_kernel,
        grid=(nb,),
        in_specs=specs(fwd) + specs(bwd) + small,
        out_specs=[pl.BlockSpec((t, ML_WIDTH), lambda s: (fwd(s), 0)),
                   pl.BlockSpec((t, ML_WIDTH), lambda s: (bwd(s), 0))],
        out_shape=[jax.ShapeDtypeStruct((r, ML_WIDTH), F32),
                   jax.ShapeDtypeStruct((r, ML_WIDTH), F32)],
        scratch_shapes=[pltpu.VMEM((2, ML_PAIRS, ML_DQK_PAD, ML_PAIR_W), F32),
                        pltpu.VMEM((2, ML_HEADS, 1, ML_DQK_PAD), F32),
                        pltpu.VMEM((2, 1, ML_HEADS), F32)],
        compiler_params=_cparams(("arbitrary",)),
        name="mlstm",
    )(p, p, p, g, gt, p, p, p, g, gt, bi, bf, bi.reshape(nh2, 1), bf.reshape(nh2, 1))


def _mlstm_out_kernel(hf_ref, hb_ref, o_ref, nw_ref, out_ref):
    h = hf_ref[...] + hb_ref[...]
    lane = lax.broadcasted_iota(jnp.int32, h.shape, 1)
    first = lane < ML_DV
    sq = h * h
    ms0 = jnp.sum(jnp.where(first, sq, 0.0), axis=1, keepdims=True) * (1.0 / ML_DV)
    ms1 = jnp.sum(jnp.where(first, 0.0, sq), axis=1, keepdims=True) * (1.0 / ML_DV)
    hn = h * jnp.where(first, lax.rsqrt(ms0 + EPS), lax.rsqrt(ms1 + EPS)) * nw_ref[...]
    out_ref[...] = (hn * jax.nn.sigmoid(o_ref[...].astype(F32))).astype(out_ref.dtype)


def _mlstm_out(hf, hb, p, ml_norm_w):
    r = hf.shape[0]
    tm = ROW_BLOCK
    return pl.pallas_call(
        _mlstm_out_kernel,
        grid=(r // tm, ML_PAIRS),
        in_specs=[pl.BlockSpec((tm, ML_PAIR_W), lambda i, j: (i, j)),
                  pl.BlockSpec((tm, ML_PAIR_W), lambda i, j: (i, j)),
                  pl.BlockSpec((tm, ML_PAIR_W), lambda i, j: (i, P_MO // ML_PAIR_W + j)),
                  pl.BlockSpec((1, ML_PAIR_W), lambda i, j: (0, j))],
        out_specs=pl.BlockSpec((tm, ML_PAIR_W), lambda i, j: (i, j)),
        out_shape=jax.ShapeDtypeStruct((r, ML_WIDTH), BF16),
        compiler_params=_cparams(("arbitrary", "arbitrary")),
        name="mlstm_out",
    )(hf, hb, p, ml_norm_w)


def _pool_kernel(n_lat, n_rows, up_ref, uc_ref, un_ref, w_ref, sc_ref, o_ref):
    tm = uc_ref.shape[0]
    i = pl.program_id(0)
    half = jnp.left_shift(1, pl.program_id(1))
    is_ctx = i * tm >= n_lat
    seq_lo = jnp.where(is_ctx, n_lat, 0)
    seq_hi = jnp.where(is_ctx, n_rows, n_lat)
    t_idx = i * tm + lax.broadcasted_iota(jnp.int32, (tm, 1), 0)
    lo = jnp.maximum(t_idx - half, seq_lo)
    hi = jnp.minimum(t_idx + half, seq_hi)
    s_idx = (i - 1) * tm + lax.broadcasted_iota(jnp.int32, (tm, 3 * tm), 1)
    band = jnp.where(s_idx >= lo, jnp.where(s_idx < hi, 1.0, 0.0), 0.0).astype(BF16)
    u_cur = uc_ref[...]
    win = (_dot(band[:, 0:tm], up_ref[...]) + _dot(band[:, tm:2 * tm], u_cur)
           + _dot(band[:, 2 * tm:3 * tm], un_ref[...]))
    pooled = win / (hi - lo).astype(F32) - u_cur.astype(F32)
    o_ref[...] = (_dot(pooled.astype(BF16), w_ref[...]) * sc_ref[...]).astype(o_ref.dtype)


def _pool(p, pool_w, pool_scale, n_lat):
    r = p.shape[0]
    tm = ROW_BLOCK
    nb = r // tm
    col = lambda g: P_PU // POOL_GC + g
    return pl.pallas_call(
        functools.partial(_pool_kernel, n_lat, r),
        grid=(nb, POOL_GROUPS),
        in_specs=[pl.BlockSpec((tm, POOL_GC), lambda i, g: (jnp.maximum(i - 1, 0), col(g))),
                  pl.BlockSpec((tm, POOL_GC), lambda i, g: (i, col(g))),
                  pl.BlockSpec((tm, POOL_GC), lambda i, g: (jnp.minimum(i + 1, nb - 1), col(g))),
                  pl.BlockSpec((None, POOL_GC, POOL_GC), lambda i, g: (g, 0, 0)),
                  pl.BlockSpec((1, POOL_GC), lambda i, g: (0, g))],
        out_specs=pl.BlockSpec((tm, POOL_GC), lambda i, g: (i, g)),
        out_shape=jax.ShapeDtypeStruct((r, POOL_WIDTH), BF16),
        compiler_params=_cparams(("arbitrary", "arbitrary")),
        name="pool",
    )(p, p, p, pool_w, pool_scale)


def _gate_kernel(a_ref, ml_ref, pl_ref, z_ref, o_ref):
    z = z_ref[...].astype(F32)
    sz = z * jax.nn.sigmoid(z)
    o_ref[:, 0:MLA_WIDTH] = (a_ref[...].astype(F32) * sz[:, 0:MLA_WIDTH]).astype(BF16)
    o_ref[:, MLA_WIDTH:MLA_WIDTH + ML_WIDTH] = (
        ml_ref[...].astype(F32) * sz[:, MLA_WIDTH:MLA_WIDTH + ML_WIDTH]).astype(BF16)
    o_ref[:, MLA_WIDTH + ML_WIDTH:MIX_WIDTH] = (
        pl_ref[...].astype(F32) * sz[:, MLA_WIDTH + ML_WIDTH:MIX_WIDTH]).astype(BF16)


def _gate(a, ml, pooled, p):
    r = a.shape[0]
    tm = ROW_BLOCK
    return pl.pallas_call(
        _gate_kernel,
        grid=(r // tm,),
        in_specs=[pl.BlockSpec((tm, MLA_WIDTH), lambda i: (i, 0)),
                  pl.BlockSpec((tm, ML_WIDTH), lambda i: (i, 0)),
                  pl.BlockSpec((tm, POOL_WIDTH), lambda i: (i, 0)),
                  pl.BlockSpec((tm, MIX_WIDTH), lambda i: (i, P_Z // MIX_WIDTH))],
        out_specs=pl.BlockSpec((tm, MIX_WIDTH), lambda i: (i, 0)),
        out_shape=jax.ShapeDtypeStruct((r, MIX_WIDTH), BF16),
        compiler_params=_cparams(("arbitrary",)),
        name="gate",
    )(a, ml, pooled, p)


def _final_norm_kernel(x_ref, w_ref, o_ref):
    o_ref[...] = _rms_rows(x_ref[...], w_ref[...])


def _final_norm(xa, w, n_lat):
    d = xa.shape[1]
    tm = ROW_BLOCK
    return pl.pallas_call(
        _final_norm_kernel,
        grid=(n_lat // tm,),
        in_specs=[pl.BlockSpec((tm, d), lambda i: (i, 0)),
                  pl.BlockSpec((1, d), lambda i: (0, 0))],
        out_specs=pl.BlockSpec((tm, d), lambda i: (i, 0)),
        out_shape=jax.ShapeDtypeStruct((n_lat, d), F32),
        compiler_params=_cparams(("arbitrary",)),
        name="final_norm",
    )(xa, w)


def _rope_tables(n_lat, n_ctx):
    rows = n_lat // GRID_W
    row = jnp.repeat(jnp.arange(rows, dtype=F32), GRID_W)
    col = jnp.tile(jnp.arange(GRID_W, dtype=F32), rows)
    n_freq = MLA_ROPE // 4
    inv = jnp.power(ROPE_BASE, -jnp.arange(n_freq, dtype=F32) / n_freq)
    ang = jnp.concatenate([row[:, None] * inv, col[:, None] * inv], axis=-1)
    cos = jnp.concatenate([jnp.cos(ang), jnp.ones((n_ctx, ROPE_HALF), F32)], axis=0)
    sin = jnp.concatenate([jnp.sin(ang), jnp.zeros((n_ctx, ROPE_HALF), F32)], axis=0)
    pad = jnp.zeros((n_lat + n_ctx, G_WIDTH - MLA_ROPE), F32)
    cos128 = jnp.concatenate([cos, cos, pad], axis=1)
    sin128 = jnp.concatenate([-sin, sin, pad], axis=1)
    return cos.T, sin.T, cos128, sin128


def _pad_heads(w):
    d = w.shape[0]
    w = w.reshape(d, ML_HEADS, ML_DQK)
    return jnp.pad(w, ((0, 0), (0, 0), (0, ML_DQK_PAD - ML_DQK))).reshape(d, ML_QK_W)


def _layer_weights(w_in, w_uq, w_ukv):
    cq, ckv, kr, mq, mk, mv, mo, mi, mf, pu, z = jnp.split(w_in, IN_OFFSETS, axis=-1)
    d = w_in.shape[0]
    wp = jnp.concatenate([mv, mo, cq, z, pu, _pad_heads(mq), _pad_heads(mk), ckv], axis=1).astype(BF16)
    ws = jnp.concatenate([kr, mi, mf, jnp.zeros((d, G_WIDTH - G_MF - 2 * ML_HEADS), w_in.dtype)],
                         axis=1).astype(BF16)
    wq_t = w_uq.reshape(MLA_Q_RANK, MLA_HEADS, MLA_DK).transpose(1, 2, 0).astype(BF16)
    wkv = w_ukv.reshape(MLA_KV_RANK, MLA_HEADS, MLA_NOPE + MLA_V)
    wk = wkv[:, :, :MLA_NOPE].transpose(1, 0, 2).astype(BF16)
    wv_t = wkv[:, :, MLA_NOPE:].transpose(1, 2, 0).astype(BF16)
    return wp, ws, ws.T, wq_t, wk, wv_t


def kernel(x, c, ctx, c_ctx, norm_w, w_mod, b_mod, w_in, q_norm_w, kv_norm_w, w_uq, w_ukv,
           ml_b_i, ml_b_f, ml_norm_w, pool_w, pool_scale, w_out, final_norm_w):
    batch, n_lat, d = x.shape
    n_ctx = ctx.shape[1]
    depth = w_in.shape[0]
    assert batch == 1 and d == D_MODEL
    assert n_lat % ROW_BLOCK == 0 and n_ctx % ROW_BLOCK == 0 and n_lat % ATT_TK == 0

    cos_t, sin_t, cos128, sin128 = _rope_tables(n_lat, n_ctx)
    c2 = jnp.concatenate([c.reshape(1, d), c_ctx.reshape(1, d), jnp.zeros((14, d), F32)], axis=0)
    mod = _modulation(c2, w_mod, b_mod)
    xa = jnp.concatenate([x[0], ctx[0]], axis=0)

    for l in range(depth):
        wp, ws, ws_t, wq_t, wk, wv_t = _layer_weights(w_in[l], w_uq[l], w_ukv[l])
        shift2 = mod[l, 0:2, 0:d].reshape(2, 1, d)
        scale2 = mod[l, 0:2, d:2 * d].reshape(2, 1, d)
        gate2 = mod[l, 0:2, 2 * d:3 * d]
        h = _modnorm(xa, norm_w[l].reshape(1, d), scale2, shift2, n_lat)
        p = _matmul(h, wp, BF16)
        g, gt = _small_proj(h, ws, ws_t)
        qt = _q_proj(p, q_norm_w[l].reshape(1, -1), wq_t, cos_t, sin_t)
        k, vt = _kv_proj(p, g, kv_norm_w[l].reshape(1, -1), wk, wv_t, cos128, sin128)
        a = _attention(qt, k, vt, n_lat)
        hf, hb = _mlstm(p, g, gt, ml_b_i[l], ml_b_f[l], n_lat)
        ml = _mlstm_out(hf, hb, p, ml_norm_w[l].reshape(1, -1))
        pooled = _pool(p, pool_w[l].astype(BF16), pool_scale[l].reshape(1, -1), n_lat)
        mix = _gate(a, ml, pooled, p)
        xa = _matmul_residual(mix, w_out[l].astype(BF16), xa, gate2, n_lat)

    return _final_norm(xa, final_norm_w.reshape(1, d), n_lat).reshape(1, n_lat, d)
```

```python
import functools

import numpy as np
import jax
import jax.numpy as jnp
from jax import lax
from jax.experimental import pallas as pl
from jax.experimental.pallas import tpu as pltpu

F32 = jnp.float32
BF16 = jnp.bfloat16

D_MODEL = 4096
GRID_W = 64
EPS = 1e-6

MLA_HEADS = 12
MLA_NOPE = 128
MLA_ROPE = 64
MLA_V = 128
MLA_DK = MLA_NOPE + MLA_ROPE
MLA_Q_RANK = 1024
MLA_KV_RANK = 512
MLA_WIDTH = MLA_HEADS * MLA_V
MLA_SCALE = MLA_DK ** -0.5
Q_SCALE = MLA_SCALE * float(np.log2(np.e))
ROPE_BASE = 10000.0
ROPE_HALF = MLA_ROPE // 2

ML_HEADS = 8
ML_DQK = 96
ML_DQK_PAD = 128
ML_DV = 192
ML_WIDTH = ML_HEADS * ML_DV
ML_PAIRS = ML_HEADS // 2
ML_PAIR_W = 2 * ML_DV
ML_QK_W = ML_HEADS * ML_DQK_PAD
ML_K_SCALE = ML_DQK ** -0.5

POOL_WINDOWS = (2, 4, 8, 16)
POOL_GROUPS = 4
POOL_GC = 256
POOL_WIDTH = POOL_GROUPS * POOL_GC
MIX_WIDTH = MLA_WIDTH + ML_WIDTH + POOL_WIDTH

IN_SIZES = (MLA_Q_RANK, MLA_KV_RANK, MLA_ROPE,
            ML_HEADS * ML_DQK, ML_HEADS * ML_DQK, ML_WIDTH, ML_WIDTH,
            2 * ML_HEADS, 2 * ML_HEADS, POOL_WIDTH, MIX_WIDTH)
IN_OFFSETS = tuple(int(v) for v in np.cumsum(IN_SIZES)[:-1])

P_MV = 0
P_MO = P_MV + ML_WIDTH
P_CQ = P_MO + ML_WIDTH
P_Z = P_CQ + MLA_Q_RANK
P_PU = P_Z + MIX_WIDTH
P_MQ = P_PU + POOL_WIDTH
P_MK = P_MQ + ML_QK_W
P_CKV = P_MK + ML_QK_W
P_WIDTH = P_CKV + MLA_KV_RANK
G_KR = 0
G_MI = MLA_ROPE
G_MF = G_MI + 2 * ML_HEADS
G_WIDTH = 128

ROW_BLOCK = 256
ATT_TK = 2048
V7X_VMEM_LIMIT = 56 * 1024 * 1024


def _pick(n, candidates):
    for c in candidates:
        if n % c == 0:
            return c
    raise ValueError(f"no tile in {candidates} divides {n}")


def _cparams(sem):
    return pltpu.CompilerParams(dimension_semantics=sem, vmem_limit_bytes=V7X_VMEM_LIMIT)


def _nt_dot(a, b):
    return lax.dot_general(a, b, (((1,), (1,)), ((), ())), preferred_element_type=F32)


def _dot(a, b):
    return jnp.dot(a, b, preferred_element_type=F32)


def _mod_kernel(c_ref, w_ref, b_ref, o_ref):
    c = c_ref[...]
    s = c * jax.nn.sigmoid(c)
    o_ref[...] = _dot(s.astype(BF16), w_ref[...].astype(BF16)) + b_ref[...]


def _modulation(c2, w_mod, b_mod):
    depth, d, n = w_mod.shape
    tn = 512
    return pl.pallas_call(
        _mod_kernel,
        grid=(depth, n // tn),
        in_specs=[pl.BlockSpec((16, d), lambda l, j: (0, 0)),
                  pl.BlockSpec((None, d, tn), lambda l, j: (l, 0, j)),
                  pl.BlockSpec((None, 1, tn), lambda l, j: (l, 0, j))],
        out_specs=pl.BlockSpec((None, 16, tn), lambda l, j: (l, 0, j)),
        out_shape=jax.ShapeDtypeStruct((depth, 16, n), F32),
        compiler_params=_cparams(("arbitrary", "arbitrary")),
        name="modulation",
    )(c2, w_mod, b_mod.reshape(depth, 1, n))


def _modnorm_kernel(x_ref, nw_ref, sc_ref, sh_ref, o_ref):
    x = x_ref[...]
    y = x * lax.rsqrt(jnp.mean(x * x, axis=-1, keepdims=True) + EPS) * nw_ref[...]
    o_ref[...] = (y * (1.0 + sc_ref[...]) + sh_ref[...]).astype(o_ref.dtype)


def _modnorm(xa, nw, scale2, shift2, n_lat):
    r, d = xa.shape
    tm = ROW_BLOCK
    lat_blocks = n_lat // tm
    sel = lambda i: (jnp.where(i >= lat_blocks, 1, 0), 0, 0)
    return pl.pallas_call(
        _modnorm_kernel,
        grid=(r // tm,),
        in_specs=[pl.BlockSpec((tm, d), lambda i: (i, 0)),
                  pl.BlockSpec((1, d), lambda i: (0, 0)),
                  pl.BlockSpec((None, 1, d), sel),
                  pl.BlockSpec((None, 1, d), sel)],
        out_specs=pl.BlockSpec((tm, d), lambda i: (i, 0)),
        out_shape=jax.ShapeDtypeStruct((r, d), BF16),
        compiler_params=_cparams(("arbitrary",)),
        name="modnorm",
    )(xa, nw, scale2, shift2)


def _mm_kernel(a_ref, b_ref, o_ref):
    o_ref[...] = _dot(a_ref[...], b_ref[...]).astype(o_ref.dtype)


def _matmul(a, b, out_dtype):
    m, k = a.shape
    n = b.shape[1]
    tm = _pick(m, (1280, 768, 256))
    tn = _pick(n, (512, 256, 128))
    return pl.pallas_call(
        _mm_kernel,
        grid=(m // tm, n // tn),
        in_specs=[pl.BlockSpec((tm, k), lambda i, j: (i, 0)),
                  pl.BlockSpec((k, tn), lambda i, j: (0, j))],
        out_specs=pl.BlockSpec((tm, tn), lambda i, j: (i, j)),
        out_shape=jax.ShapeDtypeStruct((m, n), out_dtype),
        compiler_params=_cparams(("arbitrary", "arbitrary")),
        name="in_proj",
    )(a, b)


def _gates_kernel(a_ref, w_ref, wt_ref, g_ref, gt_ref):
    a = a_ref[...]
    g_ref[...] = _dot(a, w_ref[...])
    gt_ref[...] = _nt_dot(wt_ref[...], a)


def _small_proj(a, ws, wst):
    m, k = a.shape
    tm = _pick(m, (1280, 768, 256))
    return pl.pallas_call(
        _gates_kernel,
        grid=(m // tm,),
        in_specs=[pl.BlockSpec((tm, k), lambda i: (i, 0)),
                  pl.BlockSpec((k, G_WIDTH), lambda i: (0, 0)),
                  pl.BlockSpec((G_WIDTH, k), lambda i: (0, 0))],
        out_specs=[pl.BlockSpec((tm, G_WIDTH), lambda i: (i, 0)),
                   pl.BlockSpec((G_WIDTH, tm), lambda i: (0, i))],
        out_shape=[jax.ShapeDtypeStruct((m, G_WIDTH), F32),
                   jax.ShapeDtypeStruct((G_WIDTH, m), F32)],
        compiler_params=_cparams(("arbitrary",)),
        name="small_proj",
    )(a, ws, wst)


def _mm_res_kernel(n_lat, a_ref, b_ref, x_ref, g_ref, o_ref):
    tm = a_ref.shape[0]
    y = _dot(a_ref[...], b_ref[...])
    row = pl.program_id(0) * tm + lax.broadcasted_iota(jnp.int32, y.shape, 0)
    g = jnp.where(row >= n_lat, g_ref[1:2, :], g_ref[0:1, :])
    o_ref[...] = x_ref[...] + g * y


def _matmul_residual(a, b, xa, gate2, n_lat):
    m, k = a.shape
    n = b.shape[1]
    tm = _pick(m, (1280, 768, 256))
    tn = _pick(n, (512, 256, 128))
    return pl.pallas_call(
        functools.partial(_mm_res_kernel, n_lat),
        grid=(m // tm, n // tn),
        in_specs=[pl.BlockSpec((tm, k), lambda i, j: (i, 0)),
                  pl.BlockSpec((k, tn), lambda i, j: (0, j)),
                  pl.BlockSpec((tm, tn), lambda i, j: (i, j)),
                  pl.BlockSpec((2, tn), lambda i, j: (0, j))],
        out_specs=pl.BlockSpec((tm, tn), lambda i, j: (i, j)),
        out_shape=jax.ShapeDtypeStruct((m, n), F32),
        input_output_aliases={2: 0},
        compiler_params=_cparams(("arbitrary", "arbitrary")),
        name="out_proj",
    )(a, b, xa, gate2)


def _rms_rows(x, w):
    return x * lax.rsqrt(jnp.mean(x * x, axis=-1, keepdims=True) + EPS) * w


def _q_kernel(cq_ref, nw_ref, wt_ref, cos_ref, sin_ref, o_ref, xn_ref):
    @pl.when(pl.program_id(1) == 0)
    def _():
        xn_ref[...] = _rms_rows(cq_ref[...].astype(F32), nw_ref[...]).astype(BF16)

    qt = _nt_dot(wt_ref[...], xn_ref[...])
    x1 = qt[MLA_NOPE:MLA_NOPE + ROPE_HALF]
    x2 = qt[MLA_NOPE + ROPE_HALF:MLA_DK]
    c = cos_ref[...]
    s = sin_ref[...]
    o_ref[0:MLA_NOPE, :] = (qt[0:MLA_NOPE] * Q_SCALE).astype(BF16)
    o_ref[MLA_NOPE:MLA_NOPE + ROPE_HALF, :] = ((x1 * c - x2 * s) * Q_SCALE).astype(BF16)
    o_ref[MLA_NOPE + ROPE_HALF:MLA_DK, :] = ((x1 * s + x2 * c) * Q_SCALE).astype(BF16)


def _q_proj(p, q_norm_w, wq_t, cos_t, sin_t):
    r = p.shape[0]
    tm = _pick(r, (1280, 768, 256))
    return pl.pallas_call(
        _q_kernel,
        grid=(r // tm, MLA_HEADS),
        in_specs=[pl.BlockSpec((tm, MLA_Q_RANK), lambda i, h: (i, P_CQ // MLA_Q_RANK)),
                  pl.BlockSpec((1, MLA_Q_RANK), lambda i, h: (0, 0)),
                  pl.BlockSpec((None, MLA_DK, MLA_Q_RANK), lambda i, h: (h, 0, 0)),
                  pl.BlockSpec((ROPE_HALF, tm), lambda i, h: (0, i)),
                  pl.BlockSpec((ROPE_HALF, tm), lambda i, h: (0, i))],
        out_specs=pl.BlockSpec((None, MLA_DK, tm), lambda i, h: (h, 0, i)),
        out_shape=jax.ShapeDtypeStruct((MLA_HEADS, MLA_DK, r), BF16),
        scratch_shapes=[pltpu.VMEM((tm, MLA_Q_RANK), BF16)],
        compiler_params=_cparams(("arbitrary", "arbitrary")),
        name="q_proj",
    )(p, q_norm_w, wq_t, cos_t, sin_t)


def _kv_kernel(ckv_ref, nw_ref, wk_ref, wvt_ref, g_ref, cos_ref, sin_ref, k_ref, vt_ref, xn_ref):
    @pl.when(pl.program_id(1) == 0)
    def _():
        xn_ref[...] = _rms_rows(ckv_ref[...].astype(F32), nw_ref[...]).astype(BF16)

    xn = xn_ref[...]
    k_ref[:, 0:MLA_NOPE] = _dot(xn, wk_ref[...]).astype(BF16)
    g = g_ref[...]
    lane = lax.broadcasted_iota(jnp.int32, g.shape, 1)
    partner = jnp.where(lane < ROPE_HALF,
                        pltpu.roll(g, G_WIDTH - ROPE_HALF, axis=1),
                        pltpu.roll(g, ROPE_HALF, axis=1))
    roped = g * cos_ref[...] + partner * sin_ref[...]
    k_ref[:, MLA_NOPE:MLA_DK] = roped[:, 0:MLA_ROPE].astype(BF16)
    vt_ref[...] = _nt_dot(wvt_ref[...], xn).astype(BF16)


def _kv_proj(p, g, kv_norm_w, wk, wv_t, cos128, sin128):
    r = p.shape[0]
    tm = _pick(r, (1280, 768, 256))
    return pl.pallas_call(
        _kv_kernel,
        grid=(r // tm, MLA_HEADS),
        in_specs=[pl.BlockSpec((tm, MLA_KV_RANK), lambda i, h: (i, P_CKV // MLA_KV_RANK)),
                  pl.BlockSpec((1, MLA_KV_RANK), lambda i, h: (0, 0)),
                  pl.BlockSpec((None, MLA_KV_RANK, MLA_NOPE), lambda i, h: (h, 0, 0)),
                  pl.BlockSpec((None, MLA_V, MLA_KV_RANK), lambda i, h: (h, 0, 0)),
                  pl.BlockSpec((tm, G_WIDTH), lambda i, h: (i, 0)),
                  pl.BlockSpec((tm, G_WIDTH), lambda i, h: (i, 0)),
                  pl.BlockSpec((tm, G_WIDTH), lambda i, h: (i, 0))],
        out_specs=[pl.BlockSpec((None, tm, MLA_DK), lambda i, h: (h, i, 0)),
                   pl.BlockSpec((None, MLA_V, tm), lambda i, h: (h, 0, i))],
        out_shape=[jax.ShapeDtypeStruct((MLA_HEADS, r, MLA_DK), BF16),
                   jax.ShapeDtypeStruct((MLA_HEADS, MLA_V, r), BF16)],
        scratch_shapes=[pltpu.VMEM((tm, MLA_KV_RANK), BF16)],
        compiler_params=_cparams(("arbitrary", "arbitrary")),
        name="kv_proj",
    )(p, kv_norm_w, wk, wv_t, g, cos128, sin128)


def _attn_kernel(n_main, tail_start, tail_len, qt_ref, k_ref, vt_ref, *rest):
    o_ref = rest[-3]
    s_a, s_b = rest[-2:]
    tq = qt_ref.shape[1]
    qt = qt_ref[...]

    def scores(start, size):
        st = _dot(k_ref[pl.ds(start, size), :], qt)
        return st, jnp.max(st, axis=0, keepdims=True)

    def scores_into(buf, start):
        st, cmax = scores(start, ATT_TK)
        buf[...] = st
        return cmax

    def update(carry, st, cmax, start, size):
        m, l, acc = carry
        m_new = jnp.maximum(m, cmax)
        alpha = jnp.exp2(m - m_new)
        pt = jnp.exp2(st - m_new)
        l = alpha * l + jnp.sum(pt, axis=0, keepdims=True)
        acc = alpha * acc + _dot(vt_ref[:, pl.ds(start, size)], pt.astype(BF16))
        return m_new, l, acc

    carry = (jnp.full((1, tq), -jnp.inf, F32), jnp.zeros((1, tq), F32), jnp.zeros((MLA_V, tq), F32))
    if n_main > 0:
        assert n_main % 2 == 0
        def body(j, loop_carry):
            m, l, acc, cmax_a = loop_carry
            start = pl.multiple_of(2 * j * ATT_TK, ATT_TK)
            cmax_b = scores_into(s_b, start + ATT_TK)
            m, l, acc = update((m, l, acc), s_a[...], cmax_a, start, ATT_TK)
            cmax_a = scores_into(s_a, start + 2 * ATT_TK)
            m, l, acc = update((m, l, acc), s_b[...], cmax_b, start + ATT_TK, ATT_TK)
            return m, l, acc, cmax_a

        cmax_a = scores_into(s_a, 0)
        m, l, acc, cmax_a = lax.fori_loop(0, n_main // 2 - 1, body, carry + (cmax_a,))
        last = (n_main - 2) * ATT_TK
        cmax_b = scores_into(s_b, last + ATT_TK)
        carry = update((m, l, acc), s_a[...], cmax_a, last, ATT_TK)
        st_tail, cmax_tail = scores(tail_start, tail_len)
        carry = update(carry, s_b[...], cmax_b, last + ATT_TK, ATT_TK)
    else:
        st_tail, cmax_tail = scores(tail_start, tail_len)
    m, l, acc = update(carry, st_tail, cmax_tail, tail_start, tail_len)
    o_ref[...] = (acc / l).T.astype(o_ref.dtype)


def _attention(qt, k, vt, n_lat):
    heads, dk, r = qt.shape
    n_ctx = r - n_lat
    tq = ROW_BLOCK
    out_shape = jax.ShapeDtypeStruct((r, heads * MLA_V), BF16)
    score_bufs = [pltpu.VMEM((ATT_TK, tq), F32), pltpu.VMEM((ATT_TK, tq), F32)]
    a = pl.pallas_call(
        functools.partial(_attn_kernel, n_lat // ATT_TK, n_lat, n_ctx),
        grid=(heads, n_lat // tq),
        in_specs=[pl.BlockSpec((None, dk, tq), lambda h, i: (h, 0, i)),
                  pl.BlockSpec((None, r, dk), lambda h, i: (h, 0, 0)),
                  pl.BlockSpec((None, MLA_V, r), lambda h, i: (h, 0, 0))],
        out_specs=pl.BlockSpec((tq, MLA_V), lambda h, i: (i, h)),
        out_shape=out_shape,
        scratch_shapes=score_bufs,
        compiler_params=_cparams(("arbitrary", "arbitrary")),
        name="attention",
    )(qt, k, vt)
    ctx_b = n_lat // n_ctx
    q_b = n_lat // tq
    return pl.pallas_call(
        functools.partial(_attn_kernel, 0, 0, n_ctx),
        grid=(heads, n_ctx // tq),
        in_specs=[pl.BlockSpec((None, dk, tq), lambda h, i: (h, 0, q_b + i)),
                  pl.BlockSpec((None, n_ctx, dk), lambda h, i: (h, ctx_b, 0)),
                  pl.BlockSpec((None, MLA_V, n_ctx), lambda h, i: (h, 0, ctx_b)),
                  pl.BlockSpec(memory_space=pl.ANY)],
        out_specs=pl.BlockSpec((tq, MLA_V), lambda h, i: (q_b + i, h)),
        out_shape=out_shape,
        input_output_aliases={3: 0},
        scratch_shapes=score_bufs,
        compiler_params=_cparams(("arbitrary", "arbitrary")),
        name="attention_ctx",
    )(qt, k, vt, a)


def _split3_dot(a, m01, left):
    hi = a.astype(BF16)
    r1 = a - hi.astype(F32)
    mid = r1.astype(BF16)
    lo = (r1 - mid.astype(F32)).astype(BF16)
    if left:
        return _dot(m01, hi) + _dot(m01, mid) + _dot(m01, lo)
    return _dot(hi, m01) + _dot(mid, m01) + _dot(lo, m01)


def _log_sigmoid(x):
    return jnp.minimum(x, 0.0) - jnp.log1p(jnp.exp(-jnp.abs(x)))


def _mlstm_direction(d, q_ref, k_ref, v_ref, g_ref, gt_ref, bi_ref, bf_ref, bit_ref, bft_ref,
                     h_ref, c_ref, n_ref, m_ref):
    t = q_ref.shape[0]
    lo = d * ML_HEADS
    r_i = lax.broadcasted_iota(jnp.int32, (t, t), 0)
    c_i = lax.broadcasted_iota(jnp.int32, (t, t), 1)
    seen = (c_i <= r_i) if d == 0 else (c_i >= r_i)
    tri_col = jnp.where(seen, 1.0, 0.0).astype(BF16)
    seen_t = (r_i <= c_i) if d == 0 else (r_i >= c_i)
    tri_row = jnp.where(seen_t, 1.0, 0.0).astype(BF16)

    g = g_ref[...]
    gt = gt_ref[...]
    li_col = g[:, G_MI + lo:G_MI + lo + ML_HEADS] + bi_ref[:, lo:lo + ML_HEADS]
    lf_col = _log_sigmoid(g[:, G_MF + lo:G_MF + lo + ML_HEADS] + bf_ref[:, lo:lo + ML_HEADS])
    li_row = gt[G_MI + lo:G_MI + lo + ML_HEADS, :] + bit_ref[lo:lo + ML_HEADS, :]
    lf_row = _log_sigmoid(gt[G_MF + lo:G_MF + lo + ML_HEADS, :] + bft_ref[lo:lo + ML_HEADS, :])
    b_col = _split3_dot(lf_col, tri_col, left=True)
    b_row = _split3_dot(lf_row, tri_row, left=False)
    last = t - 1 if d == 0 else 0

    lane = lax.broadcasted_iota(jnp.int32, (1, ML_PAIR_W), 1)
    first = lane < ML_DV
    m_all = m_ref[d]
    m_new_all = []
    for pair in range(ML_PAIRS):
        v2 = v_ref[:, pair * ML_PAIR_W:(pair + 1) * ML_PAIR_W]
        ct2 = c_ref[d, pair]
        ct2_b = ct2.astype(BF16)
        inter_n, sv, den, dec, upd = [], [], [], [], []
        for sub in range(2):
            hd = 2 * pair + sub
            q = q_ref[:, hd * ML_DQK_PAD:(hd + 1) * ML_DQK_PAD]
            k = k_ref[:, hd * ML_DQK_PAD:(hd + 1) * ML_DQK_PAD]
            bc = b_col[:, hd:hd + 1]
            br = b_row[hd:hd + 1, :]
            lic = li_col[:, hd:hd + 1]
            lir = li_row[hd:hd + 1, :]
            m = m_all[:, hd:hd + 1]
            b_end = bc[last:last + 1, :]
            a_col = b_end - bc + lic
            m_new = jnp.maximum(b_end + m, jnp.max(a_col, axis=0, keepdims=True))
            decay = jnp.exp(b_end + m - m_new)
            w_col = jnp.exp(a_col - m_new)
            logw = jnp.where(seen, bc - br + lir, -jnp.inf)
            m_t = jnp.maximum(bc + m, jnp.max(logw, axis=1, keepdims=True))
            inter = jnp.exp(bc + m - m_t)
            s = _nt_dot(q, k) * (jnp.exp(logw - m_t) * ML_K_SCALE)
            n_row = n_ref[d, hd]
            qf = q.astype(F32)
            den_h = inter * jnp.sum(qf * n_row, axis=1, keepdims=True) + jnp.sum(s, axis=1, keepdims=True)
            den.append(jnp.maximum(jnp.abs(den_h), jnp.exp(-m_t)))
            inter_n.append(inter * _dot(q, ct2_b))
            sv.append(_dot(s.astype(BF16), v2))
            kw = k.astype(F32) * (w_col * ML_K_SCALE)
            upd.append(_dot(kw.T.astype(BF16), v2))
            dec.append(decay)
            n_ref[d, hd] = decay * n_row + jnp.sum(kw, axis=0, keepdims=True)
            m_new_all.append(m_new)
        num = jnp.where(first, inter_n[0] + sv[0], inter_n[1] + sv[1])
        h_ref[:, pair * ML_PAIR_W:(pair + 1) * ML_PAIR_W] = num / jnp.where(first, den[0], den[1])
        c_ref[d, pair] = jnp.where(first, dec[0] * ct2 + upd[0], dec[1] * ct2 + upd[1])
    head_lane = lax.broadcasted_iota(jnp.int32, (1, ML_HEADS), 1)
    m_vec = jnp.zeros((1, ML_HEADS), F32)
    for hd, m_new in enumerate(m_new_all):
        m_vec = jnp.where(head_lane == hd, m_new, m_vec)
    m_ref[d] = m_vec


def _mlstm_kernel(qf_ref, kf_ref, vf_ref, gf_ref, gtf_ref, qb_ref, kb_ref, vb_ref, gb_ref, gtb_ref,
                  bi_ref, bf_ref, bit_ref, bft_ref, hf_ref, hb_ref, c_ref, n_ref, m_ref):
    @pl.when(pl.program_id(0) == 0)
    def _():
        c_ref[...] = jnp.zeros_like(c_ref)
        n_ref[...] = jnp.zeros_like(n_ref)
        m_ref[...] = jnp.zeros_like(m_ref)

    common = (bi_ref, bf_ref, bit_ref, bft_ref)
    _mlstm_direction(0, qf_ref, kf_ref, vf_ref, gf_ref, gtf_ref, *common, hf_ref, c_ref, n_ref, m_ref)
    _mlstm_direction(1, qb_ref, kb_ref, vb_ref, gb_ref, gtb_ref, *common, hb_ref, c_ref, n_ref, m_ref)


def _mlstm(p, g, gt, b_i, b_f, n_lat):
    r = p.shape[0]
    t = ROW_BLOCK
    nb = r // t
    lat_b = n_lat // t
    ctx_b = nb - lat_b
    fwd = lambda s: jnp.where(s < ctx_b, lat_b + s, s - ctx_b)
    bwd = lambda s: jnp.where(s < ctx_b, lat_b + ctx_b - 1 - s, lat_b - 1 - (s - ctx_b))

    def specs(order):
        return [pl.BlockSpec((t, ML_QK_W), lambda s: (order(s), P_MQ // ML_QK_W)),
                pl.BlockSpec((t, ML_QK_W), lambda s: (order(s), P_MK // ML_QK_W)),
                pl.BlockSpec((t, ML_WIDTH), lambda s: (order(s), P_MV // ML_WIDTH)),
                pl.BlockSpec((t, G_WIDTH), lambda s: (order(s), 0)),
                pl.BlockSpec((G_WIDTH, t), lambda s: (0, order(s)))]

    nh2 = 2 * ML_HEADS
    bi = b_i.reshape(1, nh2)
    bf = b_f.reshape(1, nh2)
    small = [pl.BlockSpec((1, nh2), lambda s: (0, 0)), pl.BlockSpec((1, nh2), lambda s: (0, 0)),
             pl.BlockSpec((nh2, 1), lambda s: (0, 0)), pl.BlockSpec((nh2, 1), lambda s: (0, 0))]
    return pl.pallas_call(
        _mlstm_kernel,
        grid=(nb,),
        in_specs=specs(fwd) + specs(bwd) + small,
        out_specs=[pl.BlockSpec((t, ML_WIDTH), lambda s: (fwd(s), 0)),
                   pl.BlockSpec((t, ML_WIDTH), lambda s: (bwd(s), 0))],
        out_shape=[jax.ShapeDtypeStruct((r, ML_WIDTH), F32),
                   jax.ShapeDtypeStruct((r, ML_WIDTH), F32)],
        scratch_shapes=[pltpu.VMEM((2, ML_PAIRS, ML_DQK_PAD, ML_PAIR_W), F32),
                        pltpu.VMEM((2, ML_HEADS, 1, ML_DQK_PAD), F32),
                        pltpu.VMEM((2, 1, ML_HEADS), F32)],
        compiler_params=_cparams(("arbitrary",)),
        name="mlstm",
    )(p, p, p, g, gt, p, p, p, g, gt, bi, bf, bi.reshape(nh2, 1), bf.reshape(nh2, 1))


def _mlstm_out_kernel(hf_ref, hb_ref, o_ref, nw_ref, out_ref):
    h = hf_ref[...] + hb_ref[...]
    lane = lax.broadcasted_iota(jnp.int32, h.shape, 1)
    first = lane < ML_DV
    sq = h * h
    ms0 = jnp.sum(jnp.where(first, sq, 0.0), axis=1, keepdims=True) * (1.0 / ML_DV)
    ms1 = jnp.sum(jnp.where(first, 0.0, sq), axis=1, keepdims=True) * (1.0 / ML_DV)
    hn = h * jnp.where(first, lax.rsqrt(ms0 + EPS), lax.rsqrt(ms1 + EPS)) * nw_ref[...]
    out_ref[...] = (hn * jax.nn.sigmoid(o_ref[...].astype(F32))).astype(out_ref.dtype)


def _mlstm_out(hf, hb, p, ml_norm_w):
    r = hf.shape[0]
    tm = ROW_BLOCK
    return pl.pallas_call(
        _mlstm_out_kernel,
        grid=(r // tm, ML_PAIRS),
        in_specs=[pl.BlockSpec((tm, ML_PAIR_W), lambda i, j: (i, j)),
                  pl.BlockSpec((tm, ML_PAIR_W), lambda i, j: (i, j)),
                  pl.BlockSpec((tm, ML_PAIR_W), lambda i, j: (i, P_MO // ML_PAIR_W + j)),
                  pl.BlockSpec((1, ML_PAIR_W), lambda i, j: (0, j))],
        out_specs=pl.BlockSpec((tm, ML_PAIR_W), lambda i, j: (i, j)),
        out_shape=jax.ShapeDtypeStruct((r, ML_WIDTH), BF16),
        compiler_params=_cparams(("arbitrary", "arbitrary")),
        name="mlstm_out",
    )(hf, hb, p, ml_norm_w)


def _pool_kernel(n_lat, n_rows, up_ref, uc_ref, un_ref, w_ref, sc_ref, o_ref):
    tm = uc_ref.shape[0]
    i = pl.program_id(0)
    half = jnp.left_shift(1, pl.program_id(1))
    is_ctx = i * tm >= n_lat
    seq_lo = jnp.where(is_ctx, n_lat, 0)
    seq_hi = jnp.where(is_ctx, n_rows, n_lat)
    t_idx = i * tm + lax.broadcasted_iota(jnp.int32, (tm, 1), 0)
    lo = jnp.maximum(t_idx - half, seq_lo)
    hi = jnp.minimum(t_idx + half, seq_hi)
    s_idx = (i - 1) * tm + lax.broadcasted_iota(jnp.int32, (tm, 3 * tm), 1)
    band = jnp.where(s_idx >= lo, jnp.where(s_idx < hi, 1.0, 0.0), 0.0).astype(BF16)
    u_cur = uc_ref[...]
    win = (_dot(band[:, 0:tm], up_ref[...]) + _dot(band[:, tm:2 * tm], u_cur)
           + _dot(band[:, 2 * tm:3 * tm], un_ref[...]))
    pooled = win / (hi - lo).astype(F32) - u_cur.astype(F32)
    o_ref[...] = (_dot(pooled.astype(BF16), w_ref[...]) * sc_ref[...]).astype(o_ref.dtype)


def _pool(p, pool_w, pool_scale, n_lat):
    r = p.shape[0]
    tm = ROW_BLOCK
    nb = r // tm
    col = lambda g: P_PU // POOL_GC + g
    return pl.pallas_call(
        functools.partial(_pool_kernel, n_lat, r),
        grid=(nb, POOL_GROUPS),
        in_specs=[pl.BlockSpec((tm, POOL_GC), lambda i, g: (jnp.maximum(i - 1, 0), col(g))),
                  pl.BlockSpec((tm, POOL_GC), lambda i, g: (i, col(g))),
                  pl.BlockSpec((tm, POOL_GC), lambda i, g: (jnp.minimum(i + 1, nb - 1), col(g))),
                  pl.BlockSpec((None, POOL_GC, POOL_GC), lambda i, g: (g, 0, 0)),
                  pl.BlockSpec((1, POOL_GC), lambda i, g: (0, g))],
        out_specs=pl.BlockSpec((tm, POOL_GC), lambda i, g: (i, g)),
        out_shape=jax.ShapeDtypeStruct((r, POOL_WIDTH), BF16),
        compiler_params=_cparams(("arbitrary", "arbitrary")),
        name="pool",
    )(p, p, p, pool_w, pool_scale)


def _gate_kernel(a_ref, ml_ref, pl_ref, z_ref, o_ref):
    z = z_ref[...].astype(F32)
    sz = z * jax.nn.sigmoid(z)
    o_ref[:, 0:MLA_WIDTH] = (a_ref[...].astype(F32) * sz[:, 0:MLA_WIDTH]).astype(BF16)
    o_ref[:, MLA_WIDTH:MLA_WIDTH + ML_WIDTH] = (
        ml_ref[...].astype(F32) * sz[:, MLA_WIDTH:MLA_WIDTH + ML_WIDTH]).astype(BF16)
    o_ref[:, MLA_WIDTH + ML_WIDTH:MIX_WIDTH] = (
        pl_ref[...].astype(F32) * sz[:, MLA_WIDTH + ML_WIDTH:MIX_WIDTH]).astype(BF16)


def _gate(a, ml, pooled, p):
    r = a.shape[0]
    tm = ROW_BLOCK
    return pl.pallas_call(
        _gate_kernel,
        grid=(r // tm,),
        in_specs=[pl.BlockSpec((tm, MLA_WIDTH), lambda i: (i, 0)),
                  pl.BlockSpec((tm, ML_WIDTH), lambda i: (i, 0)),
                  pl.BlockSpec((tm, POOL_WIDTH), lambda i: (i, 0)),
                  pl.BlockSpec((tm, MIX_WIDTH), lambda i: (i, P_Z // MIX_WIDTH))],
        out_specs=pl.BlockSpec((tm, MIX_WIDTH), lambda i: (i, 0)),
        out_shape=jax.ShapeDtypeStruct((r, MIX_WIDTH), BF16),
        compiler_params=_cparams(("arbitrary",)),
        name="gate",
    )(a, ml, pooled, p)


def _final_norm_kernel(x_ref, w_ref, o_ref):
    o_ref[...] = _rms_rows(x_ref[...], w_ref[...])


def _final_norm(xa, w, n_lat):
    d = xa.shape[1]
    tm = ROW_BLOCK
    return pl.pallas_call(
        _final_norm_kernel,
        grid=(n_lat // tm,),
        in_specs=[pl.BlockSpec((tm, d), lambda i: (i, 0)),
                  pl.BlockSpec((1, d), lambda i: (0, 0))],
        out_specs=pl.BlockSpec((tm, d), lambda i: (i, 0)),
        out_shape=jax.ShapeDtypeStruct((n_lat, d), F32),
        compiler_params=_cparams(("arbitrary",)),
        name="final_norm",
    )(xa, w)


def _rope_tables(n_lat, n_ctx):
    rows = n_lat // GRID_W
    row = jnp.repeat(jnp.arange(rows, dtype=F32), GRID_W)
    col = jnp.tile(jnp.arange(GRID_W, dtype=F32), rows)
    n_freq = MLA_ROPE // 4
    inv = jnp.power(ROPE_BASE, -jnp.arange(n_freq, dtype=F32) / n_freq)
    ang = jnp.concatenate([row[:, None] * inv, col[:, None] * inv], axis=-1)
    cos = jnp.concatenate([jnp.cos(ang), jnp.ones((n_ctx, ROPE_HALF), F32)], axis=0)
    sin = jnp.concatenate([jnp.sin(ang), jnp.zeros((n_ctx, ROPE_HALF), F32)], axis=0)
    pad = jnp.zeros((n_lat + n_ctx, G_WIDTH - MLA_ROPE), F32)
    cos128 = jnp.concatenate([cos, cos, pad], axis=1)
    sin128 = jnp.concatenate([-sin, sin, pad], axis=1)
    return cos.T, sin.T, cos128, sin128


def _pad_heads(w):
    d = w.shape[0]
    w = w.reshape(d, ML_HEADS, ML_DQK)
    return jnp.pad(w, ((0, 0), (0, 0), (0, ML_DQK_PAD - ML_DQK))).reshape(d, ML_QK_W)


def _layer_weights(w_in, w_uq, w_ukv):
    cq, ckv, kr, mq, mk, mv, mo, mi, mf, pu, z = jnp.split(w_in, IN_OFFSETS, axis=-1)
    d = w_in.shape[0]
    wp = jnp.concatenate([mv, mo, cq, z, pu, _pad_heads(mq), _pad_heads(mk), ckv], axis=1).astype(BF16)
    ws = jnp.concatenate([kr, mi, mf, jnp.zeros((d, G_WIDTH - G_MF - 2 * ML_HEADS), w_in.dtype)],
                         axis=1).astype(BF16)
    wq_t = w_uq.reshape(MLA_Q_RANK, MLA_HEADS, MLA_DK).transpose(1, 2, 0).astype(BF16)
    wkv = w_ukv.reshape(MLA_KV_RANK, MLA_HEADS, MLA_NOPE + MLA_V)
    wk = wkv[:, :, :MLA_NOPE].transpose(1, 0, 2).astype(BF16)
    wv_t = wkv[:, :, MLA_NOPE:].transpose(1, 2, 0).astype(BF16)
    return wp, ws, ws.T, wq_t, wk, wv_t


def kernel(x, c, ctx, c_ctx, norm_w, w_mod, b_mod, w_in, q_norm_w, kv_norm_w, w_uq, w_ukv,
           ml_b_i, ml_b_f, ml_norm_w, pool_w, pool_scale, w_out, final_norm_w):
    batch, n_lat, d = x.shape
    n_ctx = ctx.shape[1]
    depth = w_in.shape[0]
    assert batch == 1 and d == D_MODEL
    assert n_lat % ROW_BLOCK == 0 and n_ctx % ROW_BLOCK == 0 and n_lat % (2 * ATT_TK) == 0

    cos_t, sin_t, cos128, sin128 = _rope_tables(n_lat, n_ctx)
    c2 = jnp.concatenate([c.reshape(1, d), c_ctx.reshape(1, d), jnp.zeros((14, d), F32)], axis=0)
    mod = _modulation(c2, w_mod, b_mod)
    xa = jnp.concatenate([x[0], ctx[0]], axis=0)

    for l in range(depth):
        wp, ws, ws_t, wq_t, wk, wv_t = _layer_weights(w_in[l], w_uq[l], w_ukv[l])
        shift2 = mod[l, 0:2, 0:d].reshape(2, 1, d)
        scale2 = mod[l, 0:2, d:2 * d].reshape(2, 1, d)
        gate2 = mod[l, 0:2, 2 * d:3 * d]
        h = _modnorm(xa, norm_w[l].reshape(1, d), scale2, shift2, n_lat)
        p = _matmul(h, wp, BF16)
        g, gt = _small_proj(h, ws, ws_t)
        qt = _q_proj(p, q_norm_w[l].reshape(1, -1), wq_t, cos_t, sin_t)
        k, vt = _kv_proj(p, g, kv_norm_w[l].reshape(1, -1), wk, wv_t, cos128, sin128)
        a = _attention(qt, k, vt, n_lat)
        hf, hb = _mlstm(p, g, gt, ml_b_i[l], ml_b_f[l], n_lat)
        ml = _mlstm_out(hf, hb, p, ml_norm_w[l].reshape(1, -1))
        pooled = _pool(p, pool_w[l].astype(BF16), pool_scale[l].reshape(1, -1), n_lat)
        mix = _gate(a, ml, pooled, p)
        xa = _matmul_residual(mix, w_out[l].astype(BF16), xa, gate2, n_lat)

    return _final_norm(xa, final_norm_w.reshape(1, d), n_lat).reshape(1, n_lat, d)
```
